```python
import jax, jax.numpy as jnp
from jax import lax
import numpy as np

D_MODEL = 2048
BATCH = 2
SEQ = 4096
DEPTH = 1

D_CONV = D_MODEL // 2
CONV_WIDTH = 31
D_POOL = D_MODEL // 2
POOL_WINDOWS = (2, 4, 8, 16)
N_POOL_GROUPS = len(POOL_WINDOWS)
POOL_GROUP_DIM = D_POOL // N_POOL_GROUPS
N_BRANCHES = 2
D_IN_PROJ = 2 * D_CONV + D_POOL + N_BRANCHES * D_MODEL
D_FF = 5632
FFN_CONV_WIDTH = 3
PLE_DIM = 256
EPS = 1e-6

kernel_name = "hybrid_conformer_pool_gated_block"


def rmsnorm(x, g):
    xf = x.astype(jnp.float32)
    y = xf * lax.rsqrt(jnp.mean(xf * xf, axis=-1, keepdims=True) + EPS) * g.astype(jnp.float32)
    return y.astype(x.dtype)


def layernorm(x, g, b):
    xf = x.astype(jnp.float32)
    mu = jnp.mean(xf, axis=-1, keepdims=True)
    var = jnp.mean(jnp.square(xf - mu), axis=-1, keepdims=True)
    y = (xf - mu) * lax.rsqrt(var + EPS) * g.astype(jnp.float32) + b.astype(jnp.float32)
    return y.astype(x.dtype)


def causal_depthwise_conv(u, w, b):
    k, c = w.shape
    y = lax.conv_general_dilated(
        u, w[:, None, :].astype(u.dtype),
        window_strides=(1,), padding=[(k - 1, 0)],
        dimension_numbers=("NWC", "WIO", "NWC"),
        feature_group_count=c)
    return y + b.astype(u.dtype)


def causal_pool_minus_self(u, window):
    uf = u.astype(jnp.float32)
    seq = uf.shape[1]
    cs = jnp.cumsum(uf, axis=1)
    shifted = jnp.pad(cs[:, :-window], ((0, 0), (window, 0), (0, 0)))
    count = jnp.minimum(jnp.arange(seq, dtype=jnp.float32) + 1.0, float(window))
    mean = (cs - shifted) / count[None, :, None]
    return (mean - uf).astype(u.dtype)


def setup_inputs(seed: int = 0) -> dict:
    key = jax.random.key(seed)
    ks = jax.random.split(key, 32)
    nrm = lambda k, shape, scale: jax.random.normal(k, shape, jnp.float32) * scale
    gain = lambda k, n: 1.0 + 0.01 * jax.random.normal(k, (n,), jnp.float32)
    return {
        "x": nrm(ks[0], (BATCH, SEQ, D_MODEL), 1.0),
        "p": nrm(ks[1], (DEPTH, BATCH, SEQ, PLE_DIM), 1.0),
        "mix_pre_g": gain(ks[2], D_MODEL),
        "w_in": nrm(ks[3], (D_MODEL, D_IN_PROJ), D_MODEL ** -0.5),
        "conv_a_w": nrm(ks[4], (CONV_WIDTH, D_CONV), CONV_WIDTH ** -0.5),
        "conv_a_b": nrm(ks[5], (D_CONV,), 0.01),
        "ln_a_g": gain(ks[6], D_CONV),
        "ln_a_b": nrm(ks[7], (D_CONV,), 0.01),
        "w_a_out": nrm(ks[8], (D_CONV, D_MODEL), D_CONV ** -0.5),
        "pool_w": nrm(ks[9], (N_POOL_GROUPS, POOL_GROUP_DIM, POOL_GROUP_DIM), POOL_GROUP_DIM ** -0.5),
        "pool_scale": 1.0 + 0.1 * jax.random.normal(ks[10], (D_POOL,), jnp.float32),
        "w_b_out": nrm(ks[11], (D_POOL, D_MODEL), D_POOL ** -0.5),
        "w_o": nrm(ks[12], (D_MODEL, D_MODEL), D_MODEL ** -0.5),
        "mix_post_g": gain(ks[13], D_MODEL),
        "ffn_pre_g": gain(ks[14], D_MODEL),
        "w_up": nrm(ks[15], (D_MODEL, 2 * D_FF), D_MODEL ** -0.5),
        "conv_f_w": nrm(ks[16], (FFN_CONV_WIDTH, 2 * D_FF), FFN_CONV_WIDTH ** -0.5),
        "conv_f_b": nrm(ks[17], (2 * D_FF,), 0.01),
        "w_down": nrm(ks[18], (D_FF, D_MODEL), D_FF ** -0.5),
        "ffn_post_g": gain(ks[19], D_MODEL),
        "w_ple": nrm(ks[20], (PLE_DIM, D_MODEL), PLE_DIM ** -0.5),
        "w_ple_gate": nrm(ks[21], (D_MODEL, D_MODEL), D_MODEL ** -0.5),
        "ple_post_g": gain(ks[22], D_MODEL),
    }


def reference(x, p, mix_pre_g, w_in, conv_a_w, conv_a_b, ln_a_g, ln_a_b, w_a_out,
              pool_w, pool_scale, w_b_out, w_o, mix_post_g, ffn_pre_g, w_up,
              conv_f_w, conv_f_b, w_down, ffn_post_g, w_ple, w_ple_gate, ple_post_g):
    bsz, seq, _ = x.shape
    for i in range(DEPTH):
        h = rmsnorm(x, mix_pre_g)
        proj = h @ w_in
        a_in, b_in, gates = jnp.split(proj, [2 * D_CONV, 2 * D_CONV + D_POOL], axis=-1)

        a_val, a_gate = jnp.split(a_in, 2, axis=-1)
        a = a_val * jax.nn.sigmoid(a_gate)
        a = causal_depthwise_conv(a, conv_a_w, conv_a_b)
        a = jax.nn.silu(layernorm(a, ln_a_g, ln_a_b))
        y_a = a @ w_a_out

        b_groups = b_in.reshape(bsz, seq, N_POOL_GROUPS, POOL_GROUP_DIM)
        pooled = jnp.stack(
            [causal_pool_minus_self(b_groups[:, :, g], w) for g, w in enumerate(POOL_WINDOWS)],
            axis=2)
        mixed = jnp.einsum("bsgc,gcd->bsgd", pooled, pool_w).reshape(bsz, seq, D_POOL)
        y_b = (mixed * pool_scale) @ w_b_out

        g_a, g_b = jnp.split(jax.nn.sigmoid(gates), N_BRANCHES, axis=-1)
        merged = g_a * y_a + g_b * y_b
        x = x + rmsnorm(merged @ w_o, mix_post_g)

        h2 = rmsnorm(x, ffn_pre_g)
        up = causal_depthwise_conv(h2 @ w_up, conv_f_w, conv_f_b)
        u_gate, u_val = jnp.split(up, 2, axis=-1)
        f = (jax.nn.gelu(u_gate, approximate=True) * u_val) @ w_down
        x = x + rmsnorm(f, ffn_post_g)

        e = (p[i] @ w_ple) * jax.nn.sigmoid(x @ w_ple_gate)
        x = x + rmsnorm(e, ple_post_g)
    return x
```

```python
import functools

import jax
import jax.numpy as jnp
from jax import lax
from jax.experimental import pallas as pl
from jax.experimental.pallas import tpu as pltpu

EPS = 1e-6
CONV_WIDTH = 31
FFN_CONV_WIDTH = 3
POOL_WINDOWS = (2, 4, 8, 16)
HALO = 32
SUBLANES = 8
LANES = 128
MIB = 1024 * 1024

F32 = jnp.float32
BF16 = jnp.bfloat16


def _rmsnorm(x, g):
    return x * lax.rsqrt(jnp.mean(x * x, axis=-1, keepdims=True) + EPS) * g


def _resident(shape):
    nd = len(shape)
    return pl.BlockSpec(shape, lambda *_: (0,) * nd, pipeline_mode=pl.Buffered(1))


def _in_proj_kernel(x_ref, g_ref, wa_ref, wg_ref, out_ref, h_ref):
    j = pl.program_id(1)

    @pl.when(j == 0)
    def _():
        h_ref[...] = _rmsnorm(x_ref[...], g_ref[...]).astype(BF16)
        val = jnp.dot(h_ref[...], wa_ref[...], preferred_element_type=F32)
        gate = jnp.dot(h_ref[...], wg_ref[...], preferred_element_type=F32)
        out_ref[...] = val * jax.nn.sigmoid(gate)

    @pl.when(j == 1)
    def _():
        out_ref[...] = jnp.dot(h_ref[...], wa_ref[...], preferred_element_type=F32)

    @pl.when(j >= 2)
    def _():
        out_ref[...] = jax.nn.sigmoid(
            jnp.dot(h_ref[...], wa_ref[...], preferred_element_type=F32))


def _in_proj(x2d, g, w_in_bf16, d_conv, tm, tn):
    m, d = x2d.shape
    n_in = w_in_bf16.shape[1]
    n_out = n_in - d_conv
    assert d_conv == tn and n_out % tn == 0 and m % tm == 0
    nj = n_out // tn

    wa_map = lambda i, j: (0, jnp.where(j == 0, 0, j + 1))
    vmem = (2 * tm * d * 4 + tm * d * 2 + 2 * d * tn * 2 + d * tn * 2
            + 2 * tm * tn * 4 + 3 * tm * tn * 4 + 4 * MIB)
    return pl.pallas_call(
        _in_proj_kernel,
        grid=(m // tm, nj),
        in_specs=[
            pl.BlockSpec((tm, d), lambda i, j: (i, 0)),
            _resident((1, d)),
            pl.BlockSpec((d, tn), wa_map),
            pl.BlockSpec((d, tn), lambda i, j: (0, 1), pipeline_mode=pl.Buffered(1)),
        ],
        out_specs=pl.BlockSpec((tm, tn), lambda i, j: (i, j)),
        out_shape=jax.ShapeDtypeStruct((m, n_out), F32),
        scratch_shapes=[pltpu.VMEM((tm, d), BF16)],
        compiler_params=pltpu.CompilerParams(
            dimension_semantics=("arbitrary", "arbitrary"),
            vmem_limit_bytes=vmem),
        name="in_proj",
    )(x2d, g, w_in_bf16, w_in_bf16)


def _mixer_kernel(ab_ref, halo_ref, ga_ref, gb_ref, x_ref, cw_ref, cb_ref,
                  lng_ref, lnb_ref, wa_ref, pw_ref, ps_ref, wb_ref, wo_ref, g_ref,
                  out_ref, buf_ref, conv_ref, pooled_ref, *, tm, seq, d_conv):
    i = pl.program_id(0)
    tiles_per_seq = seq // tm
    seq_start = (i % tiles_per_seq) == 0

    buf_ref[0:HALO, :] = jnp.where(seq_start, 0.0, halo_ref[...])
    buf_ref[HALO:, :] = ab_ref[...]

    rows = 32
    first_tap = HALO - (CONV_WIDTH - 1)
    for q in range(tm // rows):
        for c in range(d_conv // LANES):
            cs = slice(c * LANES, (c + 1) * LANES)
            acc = jnp.broadcast_to(cb_ref[:, cs], (rows, LANES))
            for k in range(CONV_WIDTH):
                r0 = q * rows + first_tap + k
                acc = acc + buf_ref[r0:r0 + rows, cs] * cw_ref[k:k + 1, cs]
            conv_ref[q * rows:(q + 1) * rows, cs] = acc

    c = conv_ref[...]
    mu = jnp.mean(c, axis=-1, keepdims=True)
    dlt = c - mu
    var = jnp.mean(dlt * dlt, axis=-1, keepdims=True)
    y = dlt * lax.rsqrt(var + EPS) * lng_ref[...] + lnb_ref[...]
    a2 = (y * jax.nn.sigmoid(y)).astype(BF16)
    y_a = jnp.dot(a2, wa_ref[...], preferred_element_type=F32)

    gdim = d_conv // len(POOL_WINDOWS)
    pos = (i % tiles_per_seq) * tm + lax.broadcasted_iota(jnp.int32, (tm, 1), 0)
    posf = pos.astype(F32) + 1.0
    for gi, w in enumerate(POOL_WINDOWS):
        count = jnp.minimum(posf, float(w))
        for c in range(gdim // LANES):
            lo = d_conv + gi * gdim + c * LANES
            cs = slice(lo, lo + LANES)
            cur = buf_ref[HALO:HALO + tm, cs]
            tot = cur
            for s in range(1, w):
                tot = tot + buf_ref[HALO - s:HALO - s + tm, cs]
            o = gi * gdim + c * LANES
            pooled_ref[:, o:o + LANES] = (tot / count - cur).astype(BF16)

    mixed = jnp.concatenate(
        [jnp.dot(pooled_ref[:, gi * gdim:(gi + 1) * gdim], pw_ref[gi],
                 preferred_element_type=F32) for gi in range(len(POOL_WINDOWS))],
        axis=-1)
    y_b = jnp.dot((mixed * ps_ref[...]).astype(BF16), wb_ref[...],
                  preferred_element_type=F32)

    merged = ga_ref[...] * y_a + gb_ref[...] * y_b
    o = jnp.dot(merged.astype(BF16), wo_ref[...], preferred_element_type=F32)
    out_ref[...] = x_ref[...] + _rmsnorm(o, g_ref[...])


def _mixer(pq, x2d, conv_w, conv_b, ln_g, ln_b, wa, pw, ps, wb, wo, g, seq, tm):
    m, d = x2d.shape
    d_conv = conv_w.shape[1]
    assert pq.shape[1] == 2 * d_conv + 2 * d and 2 * d_conv == d
    assert seq % tm == 0 and tm % HALO == 0
    halo_map = lambda i: (jnp.maximum(i * (tm // HALO) - 1, 0), 0)
    weights = (wa.size + pw.size + wb.size + wo.size) * 2
    vmem = (2 * 4 * tm * d * 4 + 2 * tm * d * 4 + weights
            + (tm + HALO) * d * 4 + tm * d_conv * 6 + 6 * tm * d * 4 + 4 * MIB)
    kernel = functools.partial(_mixer_kernel, tm=tm, seq=seq, d_conv=d_conv)
    return pl.pallas_call(
        kernel,
        grid=(m // tm,),
        in_specs=[
            pl.BlockSpec((tm, d), lambda i: (i, 0)),
            pl.BlockSpec((HALO, d), halo_map),
            pl.BlockSpec((tm, d), lambda i: (i, 1)),
            pl.BlockSpec((tm, d), lambda i: (i, 2)),
            pl.BlockSpec((tm, d), lambda i: (i, 0)),
            _resident(conv_w.shape), _resident(conv_b.shape),
            _resident(ln_g.shape), _resident(ln_b.shape),
            _resident(wa.shape), _resident(pw.shape), _resident(ps.shape),
            _resident(wb.shape), _resident(wo.shape), _resident(g.shape),
        ],
        out_specs=pl.BlockSpec((tm, d), lambda i: (i, 0)),
        out_shape=jax.ShapeDtypeStruct((m, d), F32),
        scratch_shapes=[
            pltpu.VMEM((tm + HALO, d), F32),
            pltpu.VMEM((tm, d_conv), F32),
            pltpu.VMEM((tm, d_conv), BF16),
        ],
        compiler_params=pltpu.CompilerParams(
            dimension_semantics=("arbitrary",), vmem_limit_bytes=vmem),
        name="mixer",
    )(pq, pq, pq, pq, x2d, conv_w, conv_b, ln_g, ln_b, wa, pw, ps, wb, wo, g)


def _ffn_kernel(x_ref, g_ref, wg_ref, wv_ref, cwg_ref, cwv_ref, cbg_ref, cbv_ref,
                wd_ref, pg_ref, out_ref, h_ref, acc_ref, ug_ref, uv_ref, carry_ref,
                *, tm, tf, seq):
    i = pl.program_id(0)
    j = pl.program_id(1)
    nj = pl.num_programs(1)
    seq_start = (i % (seq // tm)) == 0

    @pl.when(j == 0)
    def _():
        h_ref[...] = _rmsnorm(x_ref[...], g_ref[...]).astype(BF16)

    prev = jnp.where(seq_start, 0.0, carry_ref[j])
    ug_ref[0:SUBLANES, :] = prev[:, :tf]
    uv_ref[0:SUBLANES, :] = prev[:, tf:]
    ug_ref[SUBLANES:, :] = jnp.dot(h_ref[...], wg_ref[...], preferred_element_type=F32)
    uv_ref[SUBLANES:, :] = jnp.dot(h_ref[...], wv_ref[...], preferred_element_type=F32)
    carry_ref[j, :, 0:tf] = ug_ref[tm:tm + SUBLANES, :]
    carry_ref[j, :, tf:2 * tf] = uv_ref[tm:tm + SUBLANES, :]

    def conv3(u_ref, w_ref, b_ref):
        out = b_ref[...]
        for k in range(FFN_CONV_WIDTH):
            r0 = SUBLANES - (FFN_CONV_WIDTH - 1) + k
            out = out + u_ref[r0:r0 + tm, :] * w_ref[k:k + 1, :]
        return out

    gate = conv3(ug_ref, cwg_ref, cbg_ref)
    val = conv3(uv_ref, cwv_ref, cbv_ref)
    act = (jax.nn.gelu(gate, approximate=True) * val).astype(BF16)
    part = jnp.dot(act, wd_ref[...], preferred_element_type=F32)

    @pl.when(j == 0)
    def _():
        acc_ref[...] = part

    @pl.when(j > 0)
    def _():
        acc_ref[...] += part

    @pl.when(j == nj - 1)
    def _():
        out_ref[...] = x_ref[...] + _rmsnorm(acc_ref[...], pg_ref[...])


def _ffn(x2d, g, w_up, conv_w, conv_b, w_down, post_g, seq, tm, tf):
    m, d = x2d.shape
    d_ff = w_down.shape[0]
    assert d_ff % tf == 0 and seq % tm == 0 and m % tm == 0
    nj = d_ff // tf
    vmem = (2 * tm * d * 4 * 2 + tm * d * 2 + tm * d * 4 + 3 * 2 * d * tf * 2
            + 2 * (tm + SUBLANES) * tf * 4 + nj * SUBLANES * 2 * tf * 4
            + 6 * tm * tf * 4 + tm * d * 4 + 4 * MIB)
    kernel = functools.partial(_ffn_kernel, tm=tm, tf=tf, seq=seq)
    return pl.pallas_call(
        kernel,
        grid=(m // tm, nj),
        in_specs=[
            pl.BlockSpec((tm, d), lambda i, j: (i, 0)),
            _resident(g.shape),
            pl.BlockSpec((d, tf), lambda i, j: (0, j)),
            pl.BlockSpec((d, tf), lambda i, j: (0, j + nj)),
            pl.BlockSpec((FFN_CONV_WIDTH, tf), lambda i, j: (0, j)),
            pl.BlockSpec((FFN_CONV_WIDTH, tf), lambda i, j: (0, j + nj)),
            pl.BlockSpec((1, tf), lambda i, j: (0, j)),
            pl.BlockSpec((1, tf), lambda i, j: (0, j + nj)),
            pl.BlockSpec((tf, d), lambda i, j: (j, 0)),
            _resident(post_g.shape),
        ],
        out_specs=pl.BlockSpec((tm, d), lambda i, j: (i, 0)),
        out_shape=jax.ShapeDtypeStruct((m, d), F32),
        scratch_shapes=[
            pltpu.VMEM((tm, d), BF16),
            pltpu.VMEM((tm, d), F32),
            pltpu.VMEM((tm + SUBLANES, tf), F32),
            pltpu.VMEM((tm + SUBLANES, tf), F32),
            pltpu.VMEM((nj, SUBLANES, 2 * tf), F32),
        ],
        compiler_params=pltpu.CompilerParams(
            dimension_semantics=("arbitrary", "arbitrary"), vmem_limit_bytes=vmem),
        name="ffn",
    )(x2d, g, w_up, w_up, conv_w, conv_w, conv_b, conv_b, w_down, post_g)


def _ple_kernel(x_ref, p_ref, wp_ref, wg_ref, g_ref, out_ref):
    x = x_ref[...]
    e = jnp.dot(p_ref[...].astype(BF16), wp_ref[...], preferred_element_type=F32)
    gate = jnp.dot(x.astype(BF16), wg_ref[...], preferred_element_type=F32)
    out_ref[...] = x + _rmsnorm(e * jax.nn.sigmoid(gate), g_ref[...])


def _ple(x2d, p2d, w_ple, w_gate, g, tm):
    m, d = x2d.shape
    dp = p2d.shape[1]
    vmem = (4 * tm * d * 4 + 2 * tm * dp * 4 + (w_ple.size + w_gate.size) * 2
            + 4 * tm * d * 4 + 4 * MIB)
    return pl.pallas_call(
        _ple_kernel,
        grid=(m // tm,),
        in_specs=[
            pl.BlockSpec((tm, d), lambda i: (i, 0)),
            pl.BlockSpec((tm, dp), lambda i: (i, 0)),
            _resident(w_ple.shape), _resident(w_gate.shape), _resident(g.shape),
        ],
        out_specs=pl.BlockSpec((tm, d), lambda i: (i, 0)),
        out_shape=jax.ShapeDtypeStruct((m, d), F32),
        compiler_params=pltpu.CompilerParams(
            dimension_semantics=("arbitrary",), vmem_limit_bytes=vmem),
        name="ple",
    )(x2d, p2d, w_ple, w_gate, g)


def kernel(x, p, mix_pre_g, w_in, conv_a_w, conv_a_b, ln_a_g, ln_a_b, w_a_out, pool_w, pool_scale, w_b_out, w_o, mix_post_g, ffn_pre_g, w_up, conv_f_w, conv_f_b, w_down, ffn_post_g, w_ple, w_ple_gate, ple_post_g):
    bsz, seq, d = x.shape
    depth = p.shape[0]
    d_conv = conv_a_w.shape[1]
    m = bsz * seq
    row = lambda v: v.reshape(1, -1)

    x2d = x.reshape(m, d)
    for layer in range(depth):
        pq = _in_proj(x2d, row(mix_pre_g), w_in.astype(BF16), d_conv, tm=512, tn=1024)
        x2d = _mixer(pq, x2d, conv_a_w, row(conv_a_b), row(ln_a_g), row(ln_a_b),
                     w_a_out.astype(BF16), pool_w.astype(BF16), row(pool_scale),
                     w_b_out.astype(BF16), w_o.astype(BF16), row(mix_post_g),
                     seq=seq, tm=256)
        x2d = _ffn(x2d, row(ffn_pre_g), w_up.astype(BF16), conv_f_w, row(conv_f_b),
                   w_down.astype(BF16), row(ffn_post_g), seq=seq, tm=512, tf=512)
        x2d = _ple(x2d, p[layer].reshape(m, -1), w_ple.astype(BF16),
                   w_ple_gate.astype(BF16), row(ple_post_g), tm=512)
    return x2d.reshape(bsz, seq, d)
```

```python
import functools

import jax
import jax.numpy as jnp
from jax import lax
from jax.experimental import pallas as pl
from jax.experimental.pallas import tpu as pltpu

EPS = 1e-6
CONV_WIDTH = 31
FFN_CONV_WIDTH = 3
POOL_WINDOWS = (2, 4, 8, 16)
HALO = 32
SUBLANES = 8
LANES = 128
MIB = 1024 * 1024

F32 = jnp.float32
BF16 = jnp.bfloat16


def _rmsnorm(x, g):
    return x * lax.rsqrt(jnp.mean(x * x, axis=-1, keepdims=True) + EPS) * g


def _resident(shape):
    nd = len(shape)
    return pl.BlockSpec(shape, lambda *_: (0,) * nd, pipeline_mode=pl.Buffered(1))


def _in_proj_kernel(x_ref, g_ref, wa_ref, wg_ref, out_ref, h_ref):
    j = pl.program_id(1)

    @pl.when(j == 0)
    def _():
        h_ref[...] = _rmsnorm(x_ref[...], g_ref[...]).astype(BF16)
        val = jnp.dot(h_ref[...], wa_ref[...], preferred_element_type=F32)
        gate = jnp.dot(h_ref[...], wg_ref[...], preferred_element_type=F32)
        out_ref[...] = val * jax.nn.sigmoid(gate)

    @pl.when(j == 1)
    def _():
        out_ref[...] = jnp.dot(h_ref[...], wa_ref[...], preferred_element_type=F32)

    @pl.when(j >= 2)
    def _():
        out_ref[...] = jax.nn.sigmoid(
            jnp.dot(h_ref[...], wa_ref[...], preferred_element_type=F32))


def _in_proj(x2d, g, w_in_bf16, d_conv, tm, tn):
    m, d = x2d.shape
    n_in = w_in_bf16.shape[1]
    n_out = n_in - d_conv
    assert d_conv == tn and n_out % tn == 0 and m % tm == 0
    nj = n_out // tn

    wa_map = lambda i, j: (0, jnp.where(j == 0, 0, j + 1))
    vmem = (2 * tm * d * 4 + tm * d * 2 + 2 * d * tn * 2 + d * tn * 2
            + 2 * tm * tn * 4 + 3 * tm * tn * 4 + 4 * MIB)
    return pl.pallas_call(
        _in_proj_kernel,
        grid=(m // tm, nj),
        in_specs=[
            pl.BlockSpec((tm, d), lambda i, j: (i, 0)),
            _resident((1, d)),
            pl.BlockSpec((d, tn), wa_map),
            pl.BlockSpec((d, tn), lambda i, j: (0, 1), pipeline_mode=pl.Buffered(1)),
        ],
        out_specs=pl.BlockSpec((tm, tn), lambda i, j: (i, j)),
        out_shape=jax.ShapeDtypeStruct((m, n_out), F32),
        scratch_shapes=[pltpu.VMEM((tm, d), BF16)],
        compiler_params=pltpu.CompilerParams(
            dimension_semantics=("arbitrary", "arbitrary"),
            vmem_limit_bytes=vmem),
        name="in_proj",
    )(x2d, g, w_in_bf16, w_in_bf16)


def _mixer_kernel(ab_ref, halo_ref, ga_ref, gb_ref, x_ref, cw_ref, cb_ref,
                  lng_ref, lnb_ref, wa_ref, pw_ref, ps_ref, wb_ref, wo_ref, g_ref,
                  out_ref, buf_ref, conv_ref, pooled_ref, *, tm, seq, d_conv):
    i = pl.program_id(0)
    tiles_per_seq = seq // tm
    seq_start = (i % tiles_per_seq) == 0

    buf_ref[0:HALO, :] = jnp.where(seq_start, 0.0, halo_ref[...])
    buf_ref[HALO:, :] = ab_ref[...]

    rows = 64
    win = rows + SUBLANES
    for q in range(tm // rows):
        base = HALO + q * rows - SUBLANES
        for c in range(d_conv // LANES):
            cs = slice(c * LANES, (c + 1) * LANES)
            acc = None
            for r in range(SUBLANES):
                part = None
                for mm in range(-(-CONV_WIDTH // SUBLANES)):
                    s = SUBLANES * mm + r
                    if s >= CONV_WIDTH:
                        continue
                    k = CONV_WIDTH - 1 - s
                    lo = base - SUBLANES * mm
                    term = buf_ref[lo:lo + win, cs] * cw_ref[k:k + 1, cs]
                    part = term if part is None else part + term
                if r:
                    part = pltpu.roll(part, r, 0)
                acc = part if acc is None else acc + part
            conv_ref[q * rows:(q + 1) * rows, cs] = acc[SUBLANES:, :] + cb_ref[:, cs]

    c = conv_ref[...]
    mu = jnp.mean(c, axis=-1, keepdims=True)
    dlt = c - mu
    var = jnp.mean(dlt * dlt, axis=-1, keepdims=True)
    y = dlt * lax.rsqrt(var + EPS) * lng_ref[...] + lnb_ref[...]
    a2 = (y * jax.nn.sigmoid(y)).astype(BF16)
    y_a = jnp.dot(a2, wa_ref[...], preferred_element_type=F32)

    gdim = d_conv // len(POOL_WINDOWS)
    lead = max(POOL_WINDOWS)
    assert lead <= HALO and lead % SUBLANES == 0
    tile_pos = (i % tiles_per_seq) * tm + 1
    for q in range(tm // rows):
        posf = (tile_pos + q * rows
                + lax.broadcasted_iota(jnp.int32, (rows, LANES), 0)).astype(F32)
        for gi, w in enumerate(POOL_WINDOWS):
            count = jnp.minimum(posf, float(w))
            for c in range(gdim // LANES):
                lo = d_conv + gi * gdim + c * LANES
                r0 = HALO + q * rows - lead
                tot = buf_ref[r0:r0 + rows + lead, lo:lo + LANES]
                cur = tot[lead:, :]
                step = 1
                while step < w:
                    tot = tot + pltpu.roll(tot, step, 0)
                    step *= 2
                o = gi * gdim + c * LANES
                pooled_ref[q * rows:(q + 1) * rows, o:o + LANES] = (
                    tot[lead:, :] / count - cur).astype(BF16)

    mixed = jnp.concatenate(
        [jnp.dot(pooled_ref[:, gi * gdim:(gi + 1) * gdim], pw_ref[gi],
                 preferred_element_type=F32) for gi in range(len(POOL_WINDOWS))],
        axis=-1)
    y_b = jnp.dot((mixed * ps_ref[...]).astype(BF16), wb_ref[...],
                  preferred_element_type=F32)

    merged = ga_ref[...] * y_a + gb_ref[...] * y_b
    o = jnp.dot(merged.astype(BF16), wo_ref[...], preferred_element_type=F32)
    out_ref[...] = x_ref[...] + _rmsnorm(o, g_ref[...])


def _mixer(pq, x2d, conv_w, conv_b, ln_g, ln_b, wa, pw, ps, wb, wo, g, seq, tm):
    m, d = x2d.shape
    d_conv = conv_w.shape[1]
    assert pq.shape[1] == 2 * d_conv + 2 * d and 2 * d_conv == d
    assert seq % tm == 0 and tm % HALO == 0
    halo_map = lambda i: (jnp.maximum(i * (tm // HALO) - 1, 0), 0)
    weights = (wa.size + pw.size + wb.size + wo.size) * 2
    vmem = (2 * 4 * tm * d * 4 + 2 * tm * d * 4 + weights
            + (tm + HALO) * d * 4 + tm * d_conv * 6 + 6 * tm * d * 4 + 4 * MIB)
    kernel = functools.partial(_mixer_kernel, tm=tm, seq=seq, d_conv=d_conv)
    return pl.pallas_call(
        kernel,
        grid=(m // tm,),
        in_specs=[
            pl.BlockSpec((tm, d), lambda i: (i, 0)),
            pl.BlockSpec((HALO, d), halo_map),
            pl.BlockSpec((tm, d), lambda i: (i, 1)),
            pl.BlockSpec((tm, d), lambda i: (i, 2)),
            pl.BlockSpec((tm, d), lambda i: (i, 0)),
            _resident(conv_w.shape), _resident(conv_b.shape),
            _resident(ln_g.shape), _resident(ln_b.shape),
            _resident(wa.shape), _resident(pw.shape), _resident(ps.shape),
            _resident(wb.shape), _resident(wo.shape), _resident(g.shape),
        ],
        out_specs=pl.BlockSpec((tm, d), lambda i: (i, 0)),
        out_shape=jax.ShapeDtypeStruct((m, d), F32),
        scratch_shapes=[
            pltpu.VMEM((tm + HALO, d), F32),
            pltpu.VMEM((tm, d_conv), F32),
            pltpu.VMEM((tm, d_conv), BF16),
        ],
        compiler_params=pltpu.CompilerParams(
            dimension_semantics=("arbitrary",), vmem_limit_bytes=vmem),
        name="mixer",
    )(pq, pq, pq, pq, x2d, conv_w, conv_b, ln_g, ln_b, wa, pw, ps, wb, wo, g)


def _ffn_kernel(x_ref, g_ref, wg_ref, wv_ref, cwg_ref, cwv_ref, cbg_ref, cbv_ref,
                wd_ref, pg_ref, out_ref, h_ref, acc_ref, ug_ref, uv_ref, carry_ref,
                *, tm, tf, seq, row_chunks):
    i = pl.program_id(0)
    j = pl.program_id(1)
    nj = pl.num_programs(1)
    seq_start = (i % (seq // tm)) == 0

    @pl.when(j == 0)
    def _():
        h_ref[...] = _rmsnorm(x_ref[...], g_ref[...]).astype(BF16)
        acc_ref[...] = jnp.zeros_like(acc_ref)

    prev = jnp.where(seq_start, 0.0, carry_ref[j])
    ug_ref[0:SUBLANES, :] = prev[:, :tf]
    uv_ref[0:SUBLANES, :] = prev[:, tf:]

    def conv3(u_ref, w_ref, b_ref, row0, nrows):
        out = b_ref[...]
        for k in range(FFN_CONV_WIDTH):
            r0 = SUBLANES + row0 - (FFN_CONV_WIDTH - 1) + k
            out = out + u_ref[r0:r0 + nrows, :] * w_ref[k:k + 1, :]
        return out

    hm = tm // row_chunks
    for hb in range(row_chunks):
        rs = slice(hb * hm, (hb + 1) * hm)
        us = slice(SUBLANES + hb * hm, SUBLANES + (hb + 1) * hm)
        ug_ref[us, :] = jnp.dot(h_ref[rs, :], wg_ref[...], preferred_element_type=F32)
        uv_ref[us, :] = jnp.dot(h_ref[rs, :], wv_ref[...], preferred_element_type=F32)
        gate = conv3(ug_ref, cwg_ref, cbg_ref, hb * hm, hm)
        val = conv3(uv_ref, cwv_ref, cbv_ref, hb * hm, hm)
        act = (jax.nn.gelu(gate, approximate=True) * val).astype(BF16)
        acc_ref[rs, :] += jnp.dot(act, wd_ref[...], preferred_element_type=F32)

    carry_ref[j, :, 0:tf] = ug_ref[tm:tm + SUBLANES, :]
    carry_ref[j, :, tf:2 * tf] = uv_ref[tm:tm + SUBLANES, :]

    @pl.when(j == nj - 1)
    def _():
        out_ref[...] = x_ref[...] + _rmsnorm(acc_ref[...], pg_ref[...])


def _ffn(x2d, g, w_up, conv_w, conv_b, w_down, post_g, seq, tm, tf, row_chunks):
    m, d = x2d.shape
    d_ff = w_down.shape[0]
    assert d_ff % tf == 0 and seq % tm == 0 and m % tm == 0
    assert tm % (row_chunks * SUBLANES) == 0
    nj = d_ff // tf
    vmem = (2 * tm * d * 4 * 2 + tm * d * 2 + tm * d * 4 + 3 * 2 * d * tf * 2
            + 2 * (tm + SUBLANES) * tf * 4 + nj * SUBLANES * 2 * tf * 4
            + 6 * tm * tf * 4 + tm * d * 4 + 4 * MIB)
    kernel = functools.partial(_ffn_kernel, tm=tm, tf=tf, seq=seq, row_chunks=row_chunks)
    return pl.pallas_call(
        kernel,
        grid=(m // tm, nj),
        in_specs=[
            pl.BlockSpec((tm, d), lambda i, j: (i, 0)),
            _resident(g.shape),
            pl.BlockSpec((d, tf), lambda i, j: (0, j)),
            pl.BlockSpec((d, tf), lambda i, j: (0, j + nj)),
            pl.BlockSpec((FFN_CONV_WIDTH, tf), lambda i, j: (0, j)),
            pl.BlockSpec((FFN_CONV_WIDTH, tf), lambda i, j: (0, j + nj)),
            pl.BlockSpec((1, tf), lambda i, j: (0, j)),
            pl.BlockSpec((1, tf), lambda i, j: (0, j + nj)),
            pl.BlockSpec((tf, d), lambda i, j: (j, 0)),
            _resident(post_g.shape),
        ],
        out_specs=pl.BlockSpec((tm, d), lambda i, j: (i, 0)),
        out_shape=jax.ShapeDtypeStruct((m, d), F32),
        scratch_shapes=[
            pltpu.VMEM((tm, d), BF16),
            pltpu.VMEM((tm, d), F32),
            pltpu.VMEM((tm + SUBLANES, tf), F32),
            pltpu.VMEM((tm + SUBLANES, tf), F32),
            pltpu.VMEM((nj, SUBLANES, 2 * tf), F32),
        ],
        compiler_params=pltpu.CompilerParams(
            dimension_semantics=("arbitrary", "arbitrary"), vmem_limit_bytes=vmem),
        name="ffn",
    )(x2d, g, w_up, w_up, conv_w, conv_w, conv_b, conv_b, w_down, post_g)


def _ple_kernel(x_ref, p_ref, wp_ref, wg_ref, g_ref, out_ref):
    x = x_ref[...]
    e = jnp.dot(p_ref[...].astype(BF16), wp_ref[...], preferred_element_type=F32)
    gate = jnp.dot(x.astype(BF16), wg_ref[...], preferred_element_type=F32)
    out_ref[...] = x + _rmsnorm(e * jax.nn.sigmoid(gate), g_ref[...])


def _ple(x2d, p2d, w_ple, w_gate, g, tm):
    m, d = x2d.shape
    dp = p2d.shape[1]
    vmem = (4 * tm * d * 4 + 2 * tm * dp * 4 + (w_ple.size + w_gate.size) * 2
            + 4 * tm * d * 4 + 4 * MIB)
    return pl.pallas_call(
        _ple_kernel,
        grid=(m // tm,),
        in_specs=[
            pl.BlockSpec((tm, d), lambda i: (i, 0)),
            pl.BlockSpec((tm, dp), lambda i: (i, 0)),
            _resident(w_ple.shape), _resident(w_gate.shape), _resident(g.shape),
        ],
        out_specs=pl.BlockSpec((tm, d), lambda i: (i, 0)),
        out_shape=jax.ShapeDtypeStruct((m, d), F32),
        compiler_params=pltpu.CompilerParams(
            dimension_semantics=("arbitrary",), vmem_limit_bytes=vmem),
        name="ple",
    )(x2d, p2d, w_ple, w_gate, g)


def kernel(x, p, mix_pre_g, w_in, conv_a_w, conv_a_b, ln_a_g, ln_a_b, w_a_out, pool_w, pool_scale, w_b_out, w_o, mix_post_g, ffn_pre_g, w_up, conv_f_w, conv_f_b, w_down, ffn_post_g, w_ple, w_ple_gate, ple_post_g):
    bsz, seq, d = x.shape
    depth = p.shape[0]
    d_conv = conv_a_w.shape[1]
    m = bsz * seq
    row = lambda v: v.reshape(1, -1)

    x2d = x.reshape(m, d)
    for layer in range(depth):
        pq = _in_proj(x2d, row(mix_pre_g), w_in.astype(BF16), d_conv, tm=1024, tn=1024)
        x2d = _mixer(pq, x2d, conv_a_w, row(conv_a_b), row(ln_a_g), row(ln_a_b),
                     w_a_out.astype(BF16), pool_w.astype(BF16), row(pool_scale),
                     w_b_out.astype(BF16), w_o.astype(BF16), row(mix_post_g),
                     seq=seq, tm=256)
        x2d = _ffn(x2d, row(ffn_pre_g), w_up.astype(BF16), conv_f_w, row(conv_f_b),
                   w_down.astype(BF16), row(ffn_post_g), seq=seq, tm=512, tf=512,
                   row_chunks=2)
        x2d = _ple(x2d, p[layer].reshape(m, -1), w_ple.astype(BF16),
                   w_ple_gate.astype(BF16), row(ple_post_g), tm=512)
    return x2d.reshape(bsz, seq, d)
```

```python
import functools

import jax
import jax.numpy as jnp
from jax import lax
from jax.experimental import pallas as pl
from jax.experimental.pallas import tpu as pltpu

EPS = 1e-6
CONV_WIDTH = 31
FFN_CONV_WIDTH = 3
POOL_WINDOWS = (2, 4, 8, 16)
HALO = 32
SUBLANES = 8
LANES = 128
MIB = 1024 * 1024

F32 = jnp.float32
BF16 = jnp.bfloat16


def _rmsnorm(x, g):
    return x * lax.rsqrt(jnp.mean(x * x, axis=-1, keepdims=True) + EPS) * g


def _resident(shape):
    nd = len(shape)
    return pl.BlockSpec(shape, lambda *_: (0,) * nd, pipeline_mode=pl.Buffered(1))


def _in_proj_kernel(x_ref, g_ref, wa_ref, wg_ref, out_ref, h_ref):
    j = pl.program_id(1)

    @pl.when(j == 0)
    def _():
        h_ref[...] = _rmsnorm(x_ref[...], g_ref[...]).astype(BF16)
        val = jnp.dot(h_ref[...], wa_ref[...], preferred_element_type=F32)
        gate = jnp.dot(h_ref[...], wg_ref[...], preferred_element_type=F32)
        out_ref[...] = val * jax.nn.sigmoid(gate)

    @pl.when(j == 1)
    def _():
        out_ref[...] = jnp.dot(h_ref[...], wa_ref[...], preferred_element_type=F32)

    @pl.when(j >= 2)
    def _():
        out_ref[...] = jax.nn.sigmoid(
            jnp.dot(h_ref[...], wa_ref[...], preferred_element_type=F32))


def _in_proj(x2d, g, w_in_bf16, d_conv, tm, tn):
    m, d = x2d.shape
    n_in = w_in_bf16.shape[1]
    n_out = n_in - d_conv
    assert d_conv == tn and n_out % tn == 0 and m % tm == 0
    nj = n_out // tn

    wa_map = lambda i, j: (0, jnp.where(j == 0, 0, j + 1))
    vmem = (2 * tm * d * 4 + tm * d * 2 + 2 * d * tn * 2 + d * tn * 2
            + 2 * tm * tn * 4 + 3 * tm * tn * 4 + 4 * MIB)
    return pl.pallas_call(
        _in_proj_kernel,
        grid=(m // tm, nj),
        in_specs=[
            pl.BlockSpec((tm, d), lambda i, j: (i, 0)),
            _resident((1, d)),
            pl.BlockSpec((d, tn), wa_map),
            pl.BlockSpec((d, tn), lambda i, j: (0, 1), pipeline_mode=pl.Buffered(1)),
        ],
        out_specs=pl.BlockSpec((tm, tn), lambda i, j: (i, j)),
        out_shape=jax.ShapeDtypeStruct((m, n_out), F32),
        scratch_shapes=[pltpu.VMEM((tm, d), BF16)],
        compiler_params=pltpu.CompilerParams(
            dimension_semantics=("arbitrary", "arbitrary"),
            vmem_limit_bytes=vmem),
        name="in_proj",
    )(x2d, g, w_in_bf16, w_in_bf16)


def _mixer_kernel(ab_ref, halo_ref, ga_ref, gb_ref, x_ref, cw_ref, cb_ref,
                  lng_ref, lnb_ref, wa_ref, pw_ref, ps_ref, wb_ref, wo_ref, g_ref,
                  out_ref, buf_ref, conv_ref, pooled_ref,
                  *, tm, seq, d_conv, row_chunks):
    i = pl.program_id(0)
    tiles_per_seq = seq // tm
    seq_start = (i % tiles_per_seq) == 0
    gdim = d_conv // len(POOL_WINDOWS)

    buf_ref[0:HALO, :] = jnp.where(seq_start, 0.0, halo_ref[...])
    buf_ref[HALO:, :] = ab_ref[...]

    rows = 64
    win = rows + SUBLANES
    lead = max(POOL_WINDOWS)
    assert lead <= HALO and lead % SUBLANES == 0
    tile_pos = (i % tiles_per_seq) * tm + 1

    def conv_unit(q, c):
        base = HALO + q * rows - SUBLANES
        cs = slice(c * LANES, (c + 1) * LANES)
        acc = None
        for r in range(SUBLANES):
            part = None
            for mm in range(-(-CONV_WIDTH // SUBLANES)):
                lookback = SUBLANES * mm + r
                if lookback >= CONV_WIDTH:
                    continue
                k = CONV_WIDTH - 1 - lookback
                lo = base - SUBLANES * mm
                term = buf_ref[lo:lo + win, cs] * cw_ref[k:k + 1, cs]
                part = term if part is None else part + term
            if r:
                part = pltpu.roll(part, r, 0)
            acc = part if acc is None else acc + part
        conv_ref[q * rows:(q + 1) * rows, cs] = acc[SUBLANES:, :] + cb_ref[:, cs]

    def pool_unit(q):
        posf = (tile_pos + q * rows
                + lax.broadcasted_iota(jnp.int32, (rows, LANES), 0)).astype(F32)
        for gi, w in enumerate(POOL_WINDOWS):
            count = jnp.minimum(posf, float(w))
            for c in range(gdim // LANES):
                lo = d_conv + gi * gdim + c * LANES
                r0 = HALO + q * rows - lead
                tot = buf_ref[r0:r0 + rows + lead, lo:lo + LANES]
                cur = tot[lead:, :]
                step = 1
                while step < w:
                    tot = tot + pltpu.roll(tot, step, 0)
                    step *= 2
                col = gi * gdim + c * LANES
                pooled_ref[q * rows:(q + 1) * rows, col:col + LANES] = (
                    tot[lead:, :] / count - cur).astype(BF16)

    hm = tm // row_chunks
    for hb in range(row_chunks):
        rs = slice(hb * hm, (hb + 1) * hm)
        for q in range(hb * hm // rows, (hb + 1) * hm // rows):
            for c in range(d_conv // LANES):
                conv_unit(q, c)
            pool_unit(q)

        cv = conv_ref[rs, :]
        mu = jnp.mean(cv, axis=-1, keepdims=True)
        dlt = cv - mu
        var = jnp.mean(dlt * dlt, axis=-1, keepdims=True)
        y = dlt * lax.rsqrt(var + EPS) * lng_ref[...] + lnb_ref[...]
        a2 = (y * jax.nn.sigmoid(y)).astype(BF16)
        y_a = jnp.dot(a2, wa_ref[...], preferred_element_type=F32)

        mixed = jnp.concatenate(
            [jnp.dot(pooled_ref[rs, gi * gdim:(gi + 1) * gdim], pw_ref[gi],
                     preferred_element_type=F32) for gi in range(len(POOL_WINDOWS))],
            axis=-1)
        y_b = jnp.dot((mixed * ps_ref[...]).astype(BF16), wb_ref[...],
                      preferred_element_type=F32)

        merged = ga_ref[rs, :] * y_a + gb_ref[rs, :] * y_b
        o = jnp.dot(merged.astype(BF16), wo_ref[...], preferred_element_type=F32)
        out_ref[rs, :] = x_ref[rs, :] + _rmsnorm(o, g_ref[...])


def _mixer(pq, x2d, conv_w, conv_b, ln_g, ln_b, wa, pw, ps, wb, wo, g, seq, tm,
           row_chunks):
    m, d = x2d.shape
    d_conv = conv_w.shape[1]
    assert pq.shape[1] == 2 * d_conv + 2 * d and 2 * d_conv == d
    assert seq % tm == 0 and tm % HALO == 0 and tm % (row_chunks * 64) == 0
    halo_map = lambda i: (jnp.maximum(i * (tm // HALO) - 1, 0), 0)
    weights = (wa.size + pw.size + wb.size + wo.size) * 2
    vmem = (2 * 4 * tm * d * 4 + 2 * tm * d * 4 + weights
            + (tm + HALO) * d * 4 + tm * d_conv * 6 + 6 * tm * d * 4 + 4 * MIB)
    kernel = functools.partial(_mixer_kernel, tm=tm, seq=seq, d_conv=d_conv,
                               row_chunks=row_chunks)
    return pl.pallas_call(
        kernel,
        grid=(m // tm,),
        in_specs=[
            pl.BlockSpec((tm, d), lambda i: (i, 0)),
            pl.BlockSpec((HALO, d), halo_map),
            pl.BlockSpec((tm, d), lambda i: (i, 1)),
            pl.BlockSpec((tm, d), lambda i: (i, 2)),
            pl.BlockSpec((tm, d), lambda i: (i, 0)),
            _resident(conv_w.shape), _resident(conv_b.shape),
            _resident(ln_g.shape), _resident(ln_b.shape),
            _resident(wa.shape), _resident(pw.shape), _resident(ps.shape),
            _resident(wb.shape), _resident(wo.shape), _resident(g.shape),
        ],
        out_specs=pl.BlockSpec((tm, d), lambda i: (i, 0)),
        out_shape=jax.ShapeDtypeStruct((m, d), F32),
        scratch_shapes=[
            pltpu.VMEM((tm + HALO, d), F32),
            pltpu.VMEM((tm, d_conv), F32),
            pltpu.VMEM((tm, d_conv), BF16),
        ],
        compiler_params=pltpu.CompilerParams(
            dimension_semantics=("arbitrary",), vmem_limit_bytes=vmem),
        name="mixer",
    )(pq, pq, pq, pq, x2d, conv_w, conv_b, ln_g, ln_b, wa, pw, ps, wb, wo, g)


def _ffn_kernel(x_ref, g_ref, wg_ref, wv_ref, cwg_ref, cwv_ref, cbg_ref, cbv_ref,
                wd_ref, pg_ref, out_ref, h_ref, ug_ref, uv_ref, carry_ref,
                *, tm, tf, seq, row_chunks):
    i = pl.program_id(0)
    j = pl.program_id(1)
    nj = pl.num_programs(1)
    seq_start = (i % (seq // tm)) == 0

    @pl.when(j == 0)
    def _():
        h_ref[...] = _rmsnorm(x_ref[...], g_ref[...]).astype(BF16)
        out_ref[...] = jnp.zeros_like(out_ref)

    prev = jnp.where(seq_start, 0.0, carry_ref[j])
    ug_ref[0:SUBLANES, :] = prev[:, :tf]
    uv_ref[0:SUBLANES, :] = prev[:, tf:]

    def conv3(u_ref, w_ref, b_ref, row0, nrows):
        out = b_ref[...]
        for k in range(FFN_CONV_WIDTH):
            r0 = SUBLANES + row0 - (FFN_CONV_WIDTH - 1) + k
            out = out + u_ref[r0:r0 + nrows, :] * w_ref[k:k + 1, :]
        return out

    hm = tm // row_chunks
    for hb in range(row_chunks):
        rs = slice(hb * hm, (hb + 1) * hm)
        us = slice(SUBLANES + hb * hm, SUBLANES + (hb + 1) * hm)
        ug_ref[us, :] = jnp.dot(h_ref[rs, :], wg_ref[...], preferred_element_type=F32)
        uv_ref[us, :] = jnp.dot(h_ref[rs, :], wv_ref[...], preferred_element_type=F32)
        gate = conv3(ug_ref, cwg_ref, cbg_ref, hb * hm, hm)
        val = conv3(uv_ref, cwv_ref, cbv_ref, hb * hm, hm)
        act = (jax.nn.gelu(gate, approximate=True) * val).astype(BF16)
        out_ref[rs, :] += jnp.dot(act, wd_ref[...], preferred_element_type=F32)

    carry_ref[j, :, 0:tf] = ug_ref[tm:tm + SUBLANES, :]
    carry_ref[j, :, tf:2 * tf] = uv_ref[tm:tm + SUBLANES, :]

    @pl.when(j == nj - 1)
    def _():
        out_ref[...] = x_ref[...] + _rmsnorm(out_ref[...], pg_ref[...])


def _ffn(x2d, g, w_up, conv_w, conv_b, w_down, post_g, seq, tm, tf, row_chunks):
    m, d = x2d.shape
    d_ff = w_down.shape[0]
    assert d_ff % tf == 0 and seq % tm == 0 and m % tm == 0
    assert tm % (row_chunks * SUBLANES) == 0
    nj = d_ff // tf
    hm = tm // row_chunks
    vmem = (3 * tm * d * 4 + tm * d * 2 + 3 * 2 * d * tf * 2
            + 2 * (tm + SUBLANES) * tf * 4 + nj * SUBLANES * 2 * tf * 4
            + 8 * hm * tf * 4 + hm * d * 4 + 4 * MIB)
    kernel = functools.partial(_ffn_kernel, tm=tm, tf=tf, seq=seq, row_chunks=row_chunks)
    return pl.pallas_call(
        kernel,
        grid=(m // tm, nj),
        in_specs=[
            pl.BlockSpec((tm, d), lambda i, j: (i, 0), pipeline_mode=pl.Buffered(1)),
            _resident(g.shape),
            pl.BlockSpec((d, tf), lambda i, j: (0, j)),
            pl.BlockSpec((d, tf), lambda i, j: (0, j + nj)),
            pl.BlockSpec((FFN_CONV_WIDTH, tf), lambda i, j: (0, j)),
            pl.BlockSpec((FFN_CONV_WIDTH, tf), lambda i, j: (0, j + nj)),
            pl.BlockSpec((1, tf), lambda i, j: (0, j)),
            pl.BlockSpec((1, tf), lambda i, j: (0, j + nj)),
            pl.BlockSpec((tf, d), lambda i, j: (j, 0)),
            _resident(post_g.shape),
        ],
        out_specs=pl.BlockSpec((tm, d), lambda i, j: (i, 0)),
        out_shape=jax.ShapeDtypeStruct((m, d), F32),
        scratch_shapes=[
            pltpu.VMEM((tm, d), BF16),
            pltpu.VMEM((tm + SUBLANES, tf), F32),
            pltpu.VMEM((tm + SUBLANES, tf), F32),
            pltpu.VMEM((nj, SUBLANES, 2 * tf), F32),
        ],
        compiler_params=pltpu.CompilerParams(
            dimension_semantics=("arbitrary", "arbitrary"), vmem_limit_bytes=vmem),
        name="ffn",
    )(x2d, g, w_up, w_up, conv_w, conv_w, conv_b, conv_b, w_down, post_g)


def _ple_kernel(x_ref, p_ref, wp_ref, wg_ref, g_ref, out_ref):
    x = x_ref[...]
    e = jnp.dot(p_ref[...].astype(BF16), wp_ref[...], preferred_element_type=F32)
    gate = jnp.dot(x.astype(BF16), wg_ref[...], preferred_element_type=F32)
    out_ref[...] = x + _rmsnorm(e * jax.nn.sigmoid(gate), g_ref[...])


def _ple(x2d, p2d, w_ple, w_gate, g, tm):
    m, d = x2d.shape
    dp = p2d.shape[1]
    vmem = (4 * tm * d * 4 + 2 * tm * dp * 4 + (w_ple.size + w_gate.size) * 2
            + 4 * tm * d * 4 + 4 * MIB)
    return pl.pallas_call(
        _ple_kernel,
        grid=(m // tm,),
        in_specs=[
            pl.BlockSpec((tm, d), lambda i: (i, 0)),
            pl.BlockSpec((tm, dp), lambda i: (i, 0)),
            _resident(w_ple.shape), _resident(w_gate.shape), _resident(g.shape),
        ],
        out_specs=pl.BlockSpec((tm, d), lambda i: (i, 0)),
        out_shape=jax.ShapeDtypeStruct((m, d), F32),
        compiler_params=pltpu.CompilerParams(
            dimension_semantics=("arbitrary",), vmem_limit_bytes=vmem),
        name="ple",
    )(x2d, p2d, w_ple, w_gate, g)


def kernel(x, p, mix_pre_g, w_in, conv_a_w, conv_a_b, ln_a_g, ln_a_b, w_a_out, pool_w, pool_scale, w_b_out, w_o, mix_post_g, ffn_pre_g, w_up, conv_f_w, conv_f_b, w_down, ffn_post_g, w_ple, w_ple_gate, ple_post_g):
    bsz, seq, d = x.shape
    depth = p.shape[0]
    d_conv = conv_a_w.shape[1]
    m = bsz * seq
    row = lambda v: v.reshape(1, -1)

    x2d = x.reshape(m, d)
    for layer in range(depth):
        pq = _in_proj(x2d, row(mix_pre_g), w_in.astype(BF16), d_conv, tm=1024, tn=1024)
        x2d = _mixer(pq, x2d, conv_a_w, row(conv_a_b), row(ln_a_g), row(ln_a_b),
                     w_a_out.astype(BF16), pool_w.astype(BF16), row(pool_scale),
                     w_b_out.astype(BF16), w_o.astype(BF16), row(mix_post_g),
                     seq=seq, tm=256, row_chunks=1)
        x2d = _ffn(x2d, row(ffn_pre_g), w_up.astype(BF16), conv_f_w, row(conv_f_b),
                   w_down.astype(BF16), row(ffn_post_g), seq=seq, tm=1024, tf=512,
                   row_chunks=2)
        x2d = _ple(x2d, p[layer].reshape(m, -1), w_ple.astype(BF16),
                   w_ple_gate.astype(BF16), row(ple_post_g), tm=512)
    return x2d.reshape(bsz, seq, d)
```

```python
import functools

import jax
import jax.numpy as jnp
from jax import lax
from jax.experimental import pallas as pl
from jax.experimental.pallas import tpu as pltpu

EPS = 1e-6
CONV_WIDTH = 31
FFN_CONV_WIDTH = 3
POOL_WINDOWS = (2, 4, 8, 16)
HALO = 32
SUBLANES = 8
LANES = 128
MIB = 1024 * 1024

F32 = jnp.float32
BF16 = jnp.bfloat16


def _rmsnorm(x, g):
    return x * lax.rsqrt(jnp.mean(x * x, axis=-1, keepdims=True) + EPS) * g


def _resident(shape):
    nd = len(shape)
    return pl.BlockSpec(shape, lambda *_: (0,) * nd, pipeline_mode=pl.Buffered(1))


def _side_cast_specs(arrays, n_blocks, index_map):
    specs, shapes, vmem = [], [], 0
    for a in arrays:
        rows, rem = divmod(a.shape[0], n_blocks)
        assert a.ndim == 2 and rem == 0 and rows % (2 * SUBLANES) == 0, a.shape
        specs.append(pl.BlockSpec((rows, a.shape[1]), index_map))
        shapes.append(jax.ShapeDtypeStruct(a.shape, BF16))
        vmem += 2 * rows * a.shape[1] * (4 + 2)
    return specs, shapes, vmem


def _side_cast(in_refs, out_refs):
    for src, dst in zip(in_refs, out_refs):
        dst[...] = src[...].astype(BF16)


def _in_proj_kernel(x_ref, g_ref, wa_ref, wg_ref, *refs, n_side, side_blocks):
    side_in, out_ref, side_out, h_ref = (
        refs[:n_side], refs[n_side], refs[n_side + 1:2 * n_side + 1], refs[2 * n_side + 1])
    i = pl.program_id(0)
    j = pl.program_id(1)

    @pl.when(i * pl.num_programs(1) + j < side_blocks)
    def _():
        _side_cast(side_in, side_out)

    @pl.when(j == 0)
    def _():
        h_ref[...] = _rmsnorm(x_ref[...], g_ref[...]).astype(BF16)
        val = jnp.dot(h_ref[...], wa_ref[...], preferred_element_type=F32)
        gate = jnp.dot(h_ref[...], wg_ref[...], preferred_element_type=F32)
        out_ref[...] = val * jax.nn.sigmoid(gate)

    @pl.when(j == 1)
    def _():
        out_ref[...] = jnp.dot(h_ref[...], wa_ref[...], preferred_element_type=F32)

    @pl.when(j >= 2)
    def _():
        out_ref[...] = jax.nn.sigmoid(
            jnp.dot(h_ref[...], wa_ref[...], preferred_element_type=F32))


def _in_proj(x2d, g, w_in_bf16, d_conv, side, tm, tn, side_blocks):
    m, d = x2d.shape
    n_in = w_in_bf16.shape[1]
    n_out = n_in - d_conv
    assert d_conv == tn and n_out % tn == 0 and m % tm == 0
    nj = n_out // tn
    assert side_blocks <= (m // tm) * nj

    wa_map = lambda i, j: (0, jnp.where(j == 0, 0, j + 1))
    side_map = lambda i, j: (jnp.minimum(i * nj + j, side_blocks - 1), 0)
    side_specs, side_shapes, side_vmem = _side_cast_specs(side, side_blocks, side_map)
    vmem = (2 * tm * d * 4 + tm * d * 2 + 2 * d * tn * 2 + d * tn * 2
            + 2 * tm * tn * 4 + 3 * tm * tn * 4 + side_vmem + 4 * MIB)
    kernel = functools.partial(_in_proj_kernel, n_side=len(side), side_blocks=side_blocks)
    return pl.pallas_call(
        kernel,
        grid=(m // tm, nj),
        in_specs=[
            pl.BlockSpec((tm, d), lambda i, j: (i, 0)),
            _resident((1, d)),
            pl.BlockSpec((d, tn), wa_map),
            pl.BlockSpec((d, tn), lambda i, j: (0, 1), pipeline_mode=pl.Buffered(1)),
            *side_specs,
        ],
        out_specs=[pl.BlockSpec((tm, tn), lambda i, j: (i, j)), *side_specs],
        out_shape=[jax.ShapeDtypeStruct((m, n_out), F32), *side_shapes],
        scratch_shapes=[pltpu.VMEM((tm, d), BF16)],
        compiler_params=pltpu.CompilerParams(
            dimension_semantics=("arbitrary", "arbitrary"),
            vmem_limit_bytes=vmem),
        name="in_proj",
    )(x2d, g, w_in_bf16, w_in_bf16, *side)


def _mixer_kernel(ab_ref, halo_ref, ga_ref, gb_ref, x_ref, cw_ref, cb_ref,
                  lng_ref, lnb_ref, wa_ref, pw_ref, ps_ref, wb_ref, wo_ref, g_ref,
                  *refs, tm, seq, d_conv, n_side):
    side_in, out_ref, side_out = (
        refs[:n_side], refs[n_side], refs[n_side + 1:2 * n_side + 1])
    buf_ref, conv_ref, pooled_ref = refs[2 * n_side + 1:]
    i = pl.program_id(0)
    tiles_per_seq = seq // tm
    seq_start = (i % tiles_per_seq) == 0
    gdim = d_conv // len(POOL_WINDOWS)

    _side_cast(side_in, side_out)

    buf_ref[0:HALO, :] = jnp.where(seq_start, 0.0, halo_ref[...])
    buf_ref[HALO:, :] = ab_ref[...]

    rows = 64
    win = rows + SUBLANES
    lead = max(POOL_WINDOWS)
    assert lead <= HALO and lead % SUBLANES == 0
    tile_pos = (i % tiles_per_seq) * tm + 1

    def conv_unit(q, c):
        base = HALO + q * rows - SUBLANES
        cs = slice(c * LANES, (c + 1) * LANES)
        acc = None
        for r in range(SUBLANES):
            part = None
            for mm in range(-(-CONV_WIDTH // SUBLANES)):
                lookback = SUBLANES * mm + r
                if lookback >= CONV_WIDTH:
                    continue
                k = CONV_WIDTH - 1 - lookback
                lo = base - SUBLANES * mm
                term = buf_ref[lo:lo + win, cs] * cw_ref[k:k + 1, cs]
                part = term if part is None else part + term
            if r:
                part = pltpu.roll(part, r, 0)
            acc = part if acc is None else acc + part
        conv_ref[q * rows:(q + 1) * rows, cs] = acc[SUBLANES:, :] + cb_ref[:, cs]

    def pool_unit(q):
        posf = (tile_pos + q * rows
                + lax.broadcasted_iota(jnp.int32, (rows, LANES), 0)).astype(F32)
        for gi, w in enumerate(POOL_WINDOWS):
            count = jnp.minimum(posf, float(w))
            for c in range(gdim // LANES):
                lo = d_conv + gi * gdim + c * LANES
                r0 = HALO + q * rows - lead
                tot = buf_ref[r0:r0 + rows + lead, lo:lo + LANES]
                cur = tot[lead:, :]
                step = 1
                while step < w:
                    tot = tot + pltpu.roll(tot, step, 0)
                    step *= 2
                col = gi * gdim + c * LANES
                pooled_ref[q * rows:(q + 1) * rows, col:col + LANES] = (
                    tot[lead:, :] / count - cur).astype(BF16)

    for q in range(tm // rows):
        for c in range(d_conv // LANES):
            conv_unit(q, c)
        pool_unit(q)

    cv = conv_ref[...]
    mu = jnp.mean(cv, axis=-1, keepdims=True)
    dlt = cv - mu
    var = jnp.mean(dlt * dlt, axis=-1, keepdims=True)
    y = dlt * lax.rsqrt(var + EPS) * lng_ref[...] + lnb_ref[...]
    a2 = (y * jax.nn.sigmoid(y)).astype(BF16)
    y_a = jnp.dot(a2, wa_ref[...], preferred_element_type=F32)

    mixed = jnp.concatenate(
        [jnp.dot(pooled_ref[:, gi * gdim:(gi + 1) * gdim], pw_ref[gi],
                 preferred_element_type=F32) for gi in range(len(POOL_WINDOWS))],
        axis=-1)
    y_b = jnp.dot((mixed * ps_ref[...]).astype(BF16), wb_ref[...],
                  preferred_element_type=F32)

    merged = ga_ref[...] * y_a + gb_ref[...] * y_b
    o = jnp.dot(merged.astype(BF16), wo_ref[...], preferred_element_type=F32)
    out_ref[...] = x_ref[...] + _rmsnorm(o, g_ref[...])


def _mixer(pq, x2d, conv_w, conv_b, ln_g, ln_b, wa, pw, ps, wb, wo, g, side, seq, tm):
    m, d = x2d.shape
    d_conv = conv_w.shape[1]
    assert pq.shape[1] == 2 * d_conv + 2 * d and 2 * d_conv == d
    assert seq % tm == 0 and tm % HALO == 0 and tm % 64 == 0
    halo_map = lambda i: (jnp.maximum(i * (tm // HALO) - 1, 0), 0)
    n_tiles = m // tm
    side_specs, side_shapes, side_vmem = _side_cast_specs(side, n_tiles, lambda i: (i, 0))
    weights = (wa.size + pw.size + wb.size + wo.size) * 2
    vmem = (2 * 4 * tm * d * 4 + 2 * tm * d * 4 + weights
            + (tm + HALO) * d * 4 + tm * d_conv * 6 + 6 * tm * d * 4
            + side_vmem + 4 * MIB)
    kernel = functools.partial(_mixer_kernel, tm=tm, seq=seq, d_conv=d_conv,
                               n_side=len(side))
    return pl.pallas_call(
        kernel,
        grid=(n_tiles,),
        in_specs=[
            pl.BlockSpec((tm, d), lambda i: (i, 0)),
            pl.BlockSpec((HALO, d), halo_map),
            pl.BlockSpec((tm, d), lambda i: (i, 1)),
            pl.BlockSpec((tm, d), lambda i: (i, 2)),
            pl.BlockSpec((tm, d), lambda i: (i, 0)),
            _resident(conv_w.shape), _resident(conv_b.shape),
            _resident(ln_g.shape), _resident(ln_b.shape),
            _resident(wa.shape), _resident(pw.shape), _resident(ps.shape),
            _resident(wb.shape), _resident(wo.shape), _resident(g.shape),
            *side_specs,
        ],
        out_specs=[pl.BlockSpec((tm, d), lambda i: (i, 0)), *side_specs],
        out_shape=[jax.ShapeDtypeStruct((m, d), F32), *side_shapes],
        scratch_shapes=[
            pltpu.VMEM((tm + HALO, d), F32),
            pltpu.VMEM((tm, d_conv), F32),
            pltpu.VMEM((tm, d_conv), BF16),
        ],
        compiler_params=pltpu.CompilerParams(
            dimension_semantics=("arbitrary",), vmem_limit_bytes=vmem),
        name="mixer",
    )(pq, pq, pq, pq, x2d, conv_w, conv_b, ln_g, ln_b, wa, pw, ps, wb, wo, g, *side)


def _ffn_kernel(x_ref, g_ref, wg_ref, wv_ref, cwg_ref, cwv_ref, cbg_ref, cbv_ref,
                wd_ref, pg_ref, out_ref, h_ref, ug_ref, uv_ref, carry_ref,
                *, tm, tf, seq, row_chunks):
    i = pl.program_id(0)
    j = pl.program_id(1)
    nj = pl.num_programs(1)
    seq_start = (i % (seq // tm)) == 0

    @pl.when(j == 0)
    def _():
        h_ref[...] = _rmsnorm(x_ref[...], g_ref[...]).astype(BF16)
        out_ref[...] = jnp.zeros_like(out_ref)

    prev = jnp.where(seq_start, 0.0, carry_ref[j])
    ug_ref[0:SUBLANES, :] = prev[:, :tf]
    uv_ref[0:SUBLANES, :] = prev[:, tf:]

    def conv3(u_ref, w_ref, b_ref, row0, nrows):
        out = b_ref[...]
        for k in range(FFN_CONV_WIDTH):
            r0 = SUBLANES + row0 - (FFN_CONV_WIDTH - 1) + k
            out = out + u_ref[r0:r0 + nrows, :] * w_ref[k:k + 1, :]
        return out

    hm = tm // row_chunks
    for hb in range(row_chunks):
        rs = slice(hb * hm, (hb + 1) * hm)
        us = slice(SUBLANES + hb * hm, SUBLANES + (hb + 1) * hm)
        ug_ref[us, :] = jnp.dot(h_ref[rs, :], wg_ref[...], preferred_element_type=F32)
        uv_ref[us, :] = jnp.dot(h_ref[rs, :], wv_ref[...], preferred_element_type=F32)
        gate = conv3(ug_ref, cwg_ref, cbg_ref, hb * hm, hm)
        val = conv3(uv_ref, cwv_ref, cbv_ref, hb * hm, hm)
        act = (jax.nn.gelu(gate, approximate=True) * val).astype(BF16)
        out_ref[rs, :] += jnp.dot(act, wd_ref[...], preferred_element_type=F32)

    carry_ref[j, :, 0:tf] = ug_ref[tm:tm + SUBLANES, :]
    carry_ref[j, :, tf:2 * tf] = uv_ref[tm:tm + SUBLANES, :]

    @pl.when(j == nj - 1)
    def _():
        out_ref[...] = x_ref[...] + _rmsnorm(out_ref[...], pg_ref[...])


def _ffn(x2d, g, w_up, conv_w, conv_b, w_down, post_g, seq, tm, tf, row_chunks):
    m, d = x2d.shape
    d_ff = w_down.shape[0]
    assert d_ff % tf == 0 and seq % tm == 0 and m % tm == 0
    assert tm % (row_chunks * SUBLANES) == 0
    nj = d_ff // tf
    hm = tm // row_chunks
    vmem = (3 * tm * d * 4 + tm * d * 2 + 3 * 2 * d * tf * 2
            + 2 * (tm + SUBLANES) * tf * 4 + nj * SUBLANES * 2 * tf * 4
            + 8 * hm * tf * 4 + hm * d * 4 + 4 * MIB)
    kernel = functools.partial(_ffn_kernel, tm=tm, tf=tf, seq=seq, row_chunks=row_chunks)
    return pl.pallas_call(
        kernel,
        grid=(m // tm, nj),
        in_specs=[
            pl.BlockSpec((tm, d), lambda i, j: (i, 0), pipeline_mode=pl.Buffered(1)),
            _resident(g.shape),
            pl.BlockSpec((d, tf), lambda i, j: (0, j)),
            pl.BlockSpec((d, tf), lambda i, j: (0, j + nj)),
            pl.BlockSpec((FFN_CONV_WIDTH, tf), lambda i, j: (0, j)),
            pl.BlockSpec((FFN_CONV_WIDTH, tf), lambda i, j: (0, j + nj)),
            pl.BlockSpec((1, tf), lambda i, j: (0, j)),
            pl.BlockSpec((1, tf), lambda i, j: (0, j + nj)),
            pl.BlockSpec((tf, d), lambda i, j: (j, 0)),
            _resident(post_g.shape),
        ],
        out_specs=pl.BlockSpec((tm, d), lambda i, j: (i, 0)),
        out_shape=jax.ShapeDtypeStruct((m, d), F32),
        scratch_shapes=[
            pltpu.VMEM((tm, d), BF16),
            pltpu.VMEM((tm + SUBLANES, tf), F32),
            pltpu.VMEM((tm + SUBLANES, tf), F32),
            pltpu.VMEM((nj, SUBLANES, 2 * tf), F32),
        ],
        compiler_params=pltpu.CompilerParams(
            dimension_semantics=("arbitrary", "arbitrary"), vmem_limit_bytes=vmem),
        name="ffn",
    )(x2d, g, w_up, w_up, conv_w, conv_w, conv_b, conv_b, w_down, post_g)


def _ple_kernel(x_ref, p_ref, wp_ref, wg_ref, g_ref, out_ref):
    x = x_ref[...]
    e = jnp.dot(p_ref[...].astype(BF16), wp_ref[...], preferred_element_type=F32)
    gate = jnp.dot(x.astype(BF16), wg_ref[...], preferred_element_type=F32)
    out_ref[...] = x + _rmsnorm(e * jax.nn.sigmoid(gate), g_ref[...])


def _ple(x2d, p2d, w_ple, w_gate, g, tm):
    m, d = x2d.shape
    dp = p2d.shape[1]
    vmem = (4 * tm * d * 4 + 2 * tm * dp * 4 + (w_ple.size + w_gate.size) * 2
            + 4 * tm * d * 4 + 4 * MIB)
    return pl.pallas_call(
        _ple_kernel,
        grid=(m // tm,),
        in_specs=[
            pl.BlockSpec((tm, d), lambda i: (i, 0)),
            pl.BlockSpec((tm, dp), lambda i: (i, 0)),
            _resident(w_ple.shape), _resident(w_gate.shape), _resident(g.shape),
        ],
        out_specs=pl.BlockSpec((tm, d), lambda i: (i, 0)),
        out_shape=jax.ShapeDtypeStruct((m, d), F32),
        compiler_params=pltpu.CompilerParams(
            dimension_semantics=("arbitrary",), vmem_limit_bytes=vmem),
        name="ple",
    )(x2d, p2d, w_ple, w_gate, g)


def kernel(x, p, mix_pre_g, w_in, conv_a_w, conv_a_b, ln_a_g, ln_a_b, w_a_out, pool_w, pool_scale, w_b_out, w_o, mix_post_g, ffn_pre_g, w_up, conv_f_w, conv_f_b, w_down, ffn_post_g, w_ple, w_ple_gate, ple_post_g):
    bsz, seq, d = x.shape
    depth = p.shape[0]
    d_conv = conv_a_w.shape[1]
    m = bsz * seq
    row = lambda v: v.reshape(1, -1)

    x2d = x.reshape(m, d)
    for layer in range(depth):
        pq, w_down_b, w_a_out_b, w_b_out_b, w_o_b = _in_proj(
            x2d, row(mix_pre_g), w_in.astype(BF16), d_conv,
            [w_down, w_a_out, w_b_out, w_o], tm=1024, tn=1024, side_blocks=32)
        x2d, w_up_b, w_ple_gate_b = _mixer(
            pq, x2d, conv_a_w, row(conv_a_b), row(ln_a_g), row(ln_a_b),
            w_a_out_b, pool_w.astype(BF16), row(pool_scale), w_b_out_b, w_o_b,
            row(mix_post_g), [w_up, w_ple_gate], seq=seq, tm=256)
        x2d = _ffn(x2d, row(ffn_pre_g), w_up_b, conv_f_w, row(conv_f_b),
                   w_down_b, row(ffn_post_g), seq=seq, tm=1024, tf=512,
                   row_chunks=2)
        x2d = _ple(x2d, p[layer].reshape(m, -1), w_ple.astype(BF16),
                   w_ple_gate_b, row(ple_post_g), tm=512)
    return x2d.reshape(bsz, seq, d)
```

```python
import functools

import jax
import jax.numpy as jnp
from jax import lax
from jax.experimental import pallas as pl
from jax.experimental.pallas import tpu as pltpu

EPS = 1e-6
CONV_WIDTH = 31
FFN_CONV_WIDTH = 3
POOL_WINDOWS = (2, 4, 8, 16)
HALO = 32
SUBLANES = 8
LANES = 128
MIB = 1024 * 1024
NORM_ROWS = 2 * SUBLANES

F32 = jnp.float32
BF16 = jnp.bfloat16


def _rmsnorm(x, g):
    return x * lax.rsqrt(jnp.mean(x * x, axis=-1, keepdims=True) + EPS) * g


def _sigmoid(x):
    return 0.5 * jnp.tanh(0.5 * x) + 0.5


def _rmsnorm_rows(src_ref, g_ref, dst_ref, res_ref=None):
    g = g_ref[...]
    for r in range(src_ref.shape[0] // NORM_ROWS):
        rs = slice(r * NORM_ROWS, (r + 1) * NORM_ROWS)
        y = _rmsnorm(src_ref[rs, :], g)
        if res_ref is not None:
            y = res_ref[rs, :] + y
        dst_ref[rs, :] = y.astype(dst_ref.dtype)


def _resident(shape):
    nd = len(shape)
    return pl.BlockSpec(shape, lambda *_: (0,) * nd, pipeline_mode=pl.Buffered(1))


def _side_cast_specs(arrays, n_blocks, index_map):
    specs, shapes, vmem = [], [], 0
    for a in arrays:
        rows, rem = divmod(a.shape[0], n_blocks)
        assert a.ndim == 2 and rem == 0 and rows % (2 * SUBLANES) == 0, a.shape
        specs.append(pl.BlockSpec((rows, a.shape[1]), index_map))
        shapes.append(jax.ShapeDtypeStruct(a.shape, BF16))
        vmem += 2 * rows * a.shape[1] * (4 + 2)
    return specs, shapes, vmem


def _side_cast(in_refs, out_refs):
    for src, dst in zip(in_refs, out_refs):
        dst[...] = src[...].astype(BF16)


def _in_proj_kernel(x_ref, g_ref, wa_ref, wg_ref, *refs, n_side, side_blocks):
    side_in, out_ref, side_out, h_ref = (
        refs[:n_side], refs[n_side], refs[n_side + 1:2 * n_side + 1], refs[2 * n_side + 1])
    i = pl.program_id(0)
    j = pl.program_id(1)

    @pl.when(i * pl.num_programs(1) + j < side_blocks)
    def _():
        _side_cast(side_in, side_out)

    @pl.when(j == 0)
    def _():
        _rmsnorm_rows(x_ref, g_ref, h_ref)
        val = jnp.dot(h_ref[...], wa_ref[...], preferred_element_type=F32)
        gate = jnp.dot(h_ref[...], wg_ref[...], preferred_element_type=F32)
        out_ref[...] = val * _sigmoid(gate)

    @pl.when(j == 1)
    def _():
        out_ref[...] = jnp.dot(h_ref[...], wa_ref[...], preferred_element_type=F32)

    @pl.when(j >= 2)
    def _():
        out_ref[...] = _sigmoid(
            jnp.dot(h_ref[...], wa_ref[...], preferred_element_type=F32))


def _in_proj(x2d, g, w_in_bf16, d_conv, side, tm, tn, side_blocks):
    m, d = x2d.shape
    n_in = w_in_bf16.shape[1]
    n_out = n_in - d_conv
    assert d_conv == tn and n_out % tn == 0 and m % tm == 0
    assert tm % NORM_ROWS == 0
    nj = n_out // tn
    assert side_blocks <= (m // tm) * nj

    wa_map = lambda i, j: (0, jnp.where(j == 0, 0, j + 1))
    side_map = lambda i, j: (jnp.minimum(i * nj + j, side_blocks - 1), 0)
    side_specs, side_shapes, side_vmem = _side_cast_specs(side, side_blocks, side_map)
    vmem = (2 * tm * d * 4 + tm * d * 2 + 2 * d * tn * 2 + d * tn * 2
            + 2 * tm * tn * 4 + 3 * tm * tn * 4 + side_vmem + 4 * MIB)
    kernel = functools.partial(_in_proj_kernel, n_side=len(side), side_blocks=side_blocks)
    return pl.pallas_call(
        kernel,
        grid=(m // tm, nj),
        in_specs=[
            pl.BlockSpec((tm, d), lambda i, j: (i, 0)),
            _resident((1, d)),
            pl.BlockSpec((d, tn), wa_map),
            pl.BlockSpec((d, tn), lambda i, j: (0, 1), pipeline_mode=pl.Buffered(1)),
            *side_specs,
        ],
        out_specs=[pl.BlockSpec((tm, tn), lambda i, j: (i, j)), *side_specs],
        out_shape=[jax.ShapeDtypeStruct((m, n_out), F32), *side_shapes],
        scratch_shapes=[pltpu.VMEM((tm, d), BF16)],
        compiler_params=pltpu.CompilerParams(
            dimension_semantics=("arbitrary", "arbitrary"),
            vmem_limit_bytes=vmem),
        name="in_proj",
    )(x2d, g, w_in_bf16, w_in_bf16, *side)


def _mixer_kernel(ab_ref, halo_ref, ga_ref, gb_ref, x_ref, cw_ref, cb_ref,
                  lng_ref, lnb_ref, wa_ref, pw_ref, ps_ref, wb_ref, wo_ref, g_ref,
                  *refs, tm, seq, d_conv, n_side):
    side_in, out_ref, side_out = (
        refs[:n_side], refs[n_side], refs[n_side + 1:2 * n_side + 1])
    buf_ref, conv_ref, pooled_ref = refs[2 * n_side + 1:]
    i = pl.program_id(0)
    tiles_per_seq = seq // tm
    seq_start = (i % tiles_per_seq) == 0
    gdim = d_conv // len(POOL_WINDOWS)

    _side_cast(side_in, side_out)

    buf_ref[0:HALO, :] = jnp.where(seq_start, 0.0, halo_ref[...])
    buf_ref[HALO:, :] = ab_ref[...]

    rows = 64
    win = rows + SUBLANES
    lead = max(POOL_WINDOWS)
    assert lead <= HALO and lead % SUBLANES == 0
    tile_pos = (i % tiles_per_seq) * tm + 1

    def conv_unit(q, c):
        base = HALO + q * rows - SUBLANES
        cs = slice(c * LANES, (c + 1) * LANES)
        acc = None
        for r in range(SUBLANES):
            part = None
            for mm in range(-(-CONV_WIDTH // SUBLANES)):
                lookback = SUBLANES * mm + r
                if lookback >= CONV_WIDTH:
                    continue
                k = CONV_WIDTH - 1 - lookback
                lo = base - SUBLANES * mm
                term = buf_ref[lo:lo + win, cs] * cw_ref[k:k + 1, cs]
                part = term if part is None else part + term
            if r:
                part = pltpu.roll(part, r, 0)
            acc = part if acc is None else acc + part
        conv_ref[q * rows:(q + 1) * rows, cs] = acc[SUBLANES:, :] + cb_ref[:, cs]

    def pool_unit(q):
        posf = (tile_pos + q * rows
                + lax.broadcasted_iota(jnp.int32, (rows, LANES), 0)).astype(F32)
        for gi, w in enumerate(POOL_WINDOWS):
            count = jnp.minimum(posf, float(w))
            for c in range(gdim // LANES):
                lo = d_conv + gi * gdim + c * LANES
                r0 = HALO + q * rows - lead
                tot = buf_ref[r0:r0 + rows + lead, lo:lo + LANES]
                cur = tot[lead:, :]
                step = 1
                while step < w:
                    tot = tot + pltpu.roll(tot, step, 0)
                    step *= 2
                col = gi * gdim + c * LANES
                pooled_ref[q * rows:(q + 1) * rows, col:col + LANES] = (
                    tot[lead:, :] / count - cur).astype(BF16)

    for q in range(tm // rows):
        for c in range(d_conv // LANES):
            conv_unit(q, c)
        pool_unit(q)

    cv = conv_ref[...]
    mu = jnp.mean(cv, axis=-1, keepdims=True)
    dlt = cv - mu
    var = jnp.mean(dlt * dlt, axis=-1, keepdims=True)
    y = dlt * lax.rsqrt(var + EPS) * lng_ref[...] + lnb_ref[...]
    a2 = (y * _sigmoid(y)).astype(BF16)
    y_a = jnp.dot(a2, wa_ref[...], preferred_element_type=F32)

    mixed = jnp.concatenate(
        [jnp.dot(pooled_ref[:, gi * gdim:(gi + 1) * gdim], pw_ref[gi],
                 preferred_element_type=F32) for gi in range(len(POOL_WINDOWS))],
        axis=-1)
    y_b = jnp.dot((mixed * ps_ref[...]).astype(BF16), wb_ref[...],
                  preferred_element_type=F32)

    merged = ga_ref[...] * y_a + gb_ref[...] * y_b
    o = jnp.dot(merged.astype(BF16), wo_ref[...], preferred_element_type=F32)
    out_ref[...] = x_ref[...] + _rmsnorm(o, g_ref[...])


def _mixer(pq, x2d, conv_w, conv_b, ln_g, ln_b, wa, pw, ps, wb, wo, g, side, seq, tm):
    m, d = x2d.shape
    d_conv = conv_w.shape[1]
    assert pq.shape[1] == 2 * d_conv + 2 * d and 2 * d_conv == d
    assert seq % tm == 0 and tm % HALO == 0 and tm % 64 == 0
    halo_map = lambda i: (jnp.maximum(i * (tm // HALO) - 1, 0), 0)
    n_tiles = m // tm
    side_specs, side_shapes, side_vmem = _side_cast_specs(side, n_tiles, lambda i: (i, 0))
    weights = (wa.size + pw.size + wb.size + wo.size) * 2
    vmem = (2 * 4 * tm * d * 4 + 2 * tm * d * 4 + weights
            + (tm + HALO) * d * 4 + tm * d_conv * 6 + 6 * tm * d * 4
            + side_vmem + 4 * MIB)
    kernel = functools.partial(_mixer_kernel, tm=tm, seq=seq, d_conv=d_conv,
                               n_side=len(side))
    return pl.pallas_call(
        kernel,
        grid=(n_tiles,),
        in_specs=[
            pl.BlockSpec((tm, d), lambda i: (i, 0)),
            pl.BlockSpec((HALO, d), halo_map),
            pl.BlockSpec((tm, d), lambda i: (i, 1)),
            pl.BlockSpec((tm, d), lambda i: (i, 2)),
            pl.BlockSpec((tm, d), lambda i: (i, 0)),
            _resident(conv_w.shape), _resident(conv_b.shape),
            _resident(ln_g.shape), _resident(ln_b.shape),
            _resident(wa.shape), _resident(pw.shape), _resident(ps.shape),
            _resident(wb.shape), _resident(wo.shape), _resident(g.shape),
            *side_specs,
        ],
        out_specs=[pl.BlockSpec((tm, d), lambda i: (i, 0)), *side_specs],
        out_shape=[jax.ShapeDtypeStruct((m, d), F32), *side_shapes],
        scratch_shapes=[
            pltpu.VMEM((tm + HALO, d), F32),
            pltpu.VMEM((tm, d_conv), F32),
            pltpu.VMEM((tm, d_conv), BF16),
        ],
        compiler_params=pltpu.CompilerParams(
            dimension_semantics=("arbitrary",), vmem_limit_bytes=vmem),
        name="mixer",
    )(pq, pq, pq, pq, x2d, conv_w, conv_b, ln_g, ln_b, wa, pw, ps, wb, wo, g, *side)


def _ffn_kernel(x_ref, g_ref, wg_ref, wv_ref, cwg_ref, cwv_ref, cbg_ref, cbv_ref,
                wd_ref, pg_ref, out_ref, h_ref, ug_ref, uv_ref, carry_ref,
                *, tm, tf, seq, row_chunks):
    i = pl.program_id(0)
    j = pl.program_id(1)
    nj = pl.num_programs(1)
    seq_start = (i % (seq // tm)) == 0

    @pl.when(j == 0)
    def _():
        _rmsnorm_rows(x_ref, g_ref, h_ref)
        out_ref[...] = jnp.zeros_like(out_ref)

    prev = jnp.where(seq_start, 0.0, carry_ref[j])
    ug_ref[0:SUBLANES, :] = prev[:, :tf]
    uv_ref[0:SUBLANES, :] = prev[:, tf:]

    def conv3(u_ref, w_ref, b_ref, row0, nrows):
        out = b_ref[...]
        for k in range(FFN_CONV_WIDTH):
            r0 = SUBLANES + row0 - (FFN_CONV_WIDTH - 1) + k
            out = out + u_ref[r0:r0 + nrows, :] * w_ref[k:k + 1, :]
        return out

    hm = tm // row_chunks
    for hb in range(row_chunks):
        rs = slice(hb * hm, (hb + 1) * hm)
        us = slice(SUBLANES + hb * hm, SUBLANES + (hb + 1) * hm)
        ug_ref[us, :] = jnp.dot(h_ref[rs, :], wg_ref[...], preferred_element_type=F32)
        uv_ref[us, :] = jnp.dot(h_ref[rs, :], wv_ref[...], preferred_element_type=F32)
        gate = conv3(ug_ref, cwg_ref, cbg_ref, hb * hm, hm)
        val = conv3(uv_ref, cwv_ref, cbv_ref, hb * hm, hm)
        act = (jax.nn.gelu(gate, approximate=True) * val).astype(BF16)
        out_ref[rs, :] += jnp.dot(act, wd_ref[...], preferred_element_type=F32)

    carry_ref[j, :, 0:tf] = ug_ref[tm:tm + SUBLANES, :]
    carry_ref[j, :, tf:2 * tf] = uv_ref[tm:tm + SUBLANES, :]

    @pl.when(j == nj - 1)
    def _():
        _rmsnorm_rows(out_ref, pg_ref, out_ref, res_ref=x_ref)


def _ffn(x2d, g, w_up, conv_w, conv_b, w_down, post_g, seq, tm, tf, row_chunks):
    m, d = x2d.shape
    d_ff = w_down.shape[0]
    assert d_ff % tf == 0 and seq % tm == 0 and m % tm == 0
    assert tm % (row_chunks * SUBLANES) == 0 and tm % NORM_ROWS == 0
    nj = d_ff // tf
    hm = tm // row_chunks
    vmem = (3 * tm * d * 4 + tm * d * 2 + 3 * 2 * d * tf * 2
            + 2 * (tm + SUBLANES) * tf * 4 + nj * SUBLANES * 2 * tf * 4
            + 8 * hm * tf * 4 + hm * d * 4 + 4 * MIB)
    kernel = functools.partial(_ffn_kernel, tm=tm, tf=tf, seq=seq, row_chunks=row_chunks)
    return pl.pallas_call(
        kernel,
        grid=(m // tm, nj),
        in_specs=[
            pl.BlockSpec((tm, d), lambda i, j: (i, 0), pipeline_mode=pl.Buffered(1)),
            _resident(g.shape),
            pl.BlockSpec((d, tf), lambda i, j: (0, j)),
            pl.BlockSpec((d, tf), lambda i, j: (0, j + nj)),
            pl.BlockSpec((FFN_CONV_WIDTH, tf), lambda i, j: (0, j)),
            pl.BlockSpec((FFN_CONV_WIDTH, tf), lambda i, j: (0, j + nj)),
            pl.BlockSpec((1, tf), lambda i, j: (0, j)),
            pl.BlockSpec((1, tf), lambda i, j: (0, j + nj)),
            pl.BlockSpec((tf, d), lambda i, j: (j, 0)),
            _resident(post_g.shape),
        ],
        out_specs=pl.BlockSpec((tm, d), lambda i, j: (i, 0)),
        out_shape=jax.ShapeDtypeStruct((m, d), F32),
        scratch_shapes=[
            pltpu.VMEM((tm, d), BF16),
            pltpu.VMEM((tm + SUBLANES, tf), F32),
            pltpu.VMEM((tm + SUBLANES, tf), F32),
            pltpu.VMEM((nj, SUBLANES, 2 * tf), F32),
        ],
        compiler_params=pltpu.CompilerParams(
            dimension_semantics=("arbitrary", "arbitrary"), vmem_limit_bytes=vmem),
        name="ffn",
    )(x2d, g, w_up, w_up, conv_w, conv_w, conv_b, conv_b, w_down, post_g)


def _ple_kernel(x_ref, p_ref, wp_ref, wg_ref, g_ref, out_ref):
    x = x_ref[...]
    e = jnp.dot(p_ref[...].astype(BF16), wp_ref[...], preferred_element_type=F32)
    gate = jnp.dot(x.astype(BF16), wg_ref[...], preferred_element_type=F32)
    out_ref[...] = x + _rmsnorm(e * _sigmoid(gate), g_ref[...])


def _ple(x2d, p2d, w_ple, w_gate, g, tm):
    m, d = x2d.shape
    dp = p2d.shape[1]
    vmem = (4 * tm * d * 4 + 2 * tm * dp * 4 + (w_ple.size + w_gate.size) * 2
            + 4 * tm * d * 4 + 4 * MIB)
    return pl.pallas_call(
        _ple_kernel,
        grid=(m // tm,),
        in_specs=[
            pl.BlockSpec((tm, d), lambda i: (i, 0)),
            pl.BlockSpec((tm, dp), lambda i: (i, 0)),
            _resident(w_ple.shape), _resident(w_gate.shape), _resident(g.shape),
        ],
        out_specs=pl.BlockSpec((tm, d), lambda i: (i, 0)),
        out_shape=jax.ShapeDtypeStruct((m, d), F32),
        compiler_params=pltpu.CompilerParams(
            dimension_semantics=("arbitrary",), vmem_limit_bytes=vmem),
        name="ple",
    )(x2d, p2d, w_ple, w_gate, g)


def kernel(x, p, mix_pre_g, w_in, conv_a_w, conv_a_b, ln_a_g, ln_a_b, w_a_out, pool_w, pool_scale, w_b_out, w_o, mix_post_g, ffn_pre_g, w_up, conv_f_w, conv_f_b, w_down, ffn_post_g, w_ple, w_ple_gate, ple_post_g):
    bsz, seq, d = x.shape
    depth = p.shape[0]
    d_conv = conv_a_w.shape[1]
    m = bsz * seq
    row = lambda v: v.reshape(1, -1)

    x2d = x.reshape(m, d)
    for layer in range(depth):
        pq, w_down_b, w_a_out_b, w_b_out_b, w_o_b = _in_proj(
            x2d, row(mix_pre_g), w_in.astype(BF16), d_conv,
            [w_down, w_a_out, w_b_out, w_o], tm=1024, tn=1024, side_blocks=32)
        x2d, w_up_b, w_ple_gate_b = _mixer(
            pq, x2d, conv_a_w, row(conv_a_b), row(ln_a_g), row(ln_a_b),
            w_a_out_b, pool_w.astype(BF16), row(pool_scale), w_b_out_b, w_o_b,
            row(mix_post_g), [w_up, w_ple_gate], seq=seq, tm=256)
        x2d = _ffn(x2d, row(ffn_pre_g), w_up_b, conv_f_w, row(conv_f_b),
                   w_down_b, row(ffn_post_g), seq=seq, tm=1024, tf=512,
                   row_chunks=2)
        x2d = _ple(x2d, p[layer].reshape(m, -1), w_ple.astype(BF16),
                   w_ple_gate_b, row(ple_post_g), tm=512)
    return x2d.reshape(bsz, seq, d)
```
